```python
import jax, jax.numpy as jnp
from jax import lax
import numpy as np

D_MODEL = 1024
BATCH = 8
SEQ = 8192
DEPTH = 1

CHUNK = 64
RET_HEADS = 4
RET_QK_DIM = 128
RET_V_DIM = D_MODEL // RET_HEADS
RET_QK_WIDTH = RET_HEADS * RET_QK_DIM
RET_V_WIDTH = RET_HEADS * RET_V_DIM
LRU_WIDTH = D_MODEL
LRU_BLOCKS = 8
LRU_BLOCK_DIM = LRU_WIDTH // LRU_BLOCKS
CONV_WIDTH = 4
RG_LRU_C = 8.0
ROPE_BASE = 10000.0
LN_EPS = 1e-5
DEEPNORM_ALPHA = (2.0 * DEPTH) ** 0.25
DEEPNORM_BETA = (8.0 * DEPTH) ** -0.25
IN_SPLITS = (RET_QK_WIDTH, RET_QK_WIDTH, RET_V_WIDTH, RET_V_WIDTH, LRU_WIDTH, LRU_WIDTH, D_MODEL, D_MODEL)
IN_WIDTH = sum(IN_SPLITS)

kernel_name = "hybrid_retention_rglru_gated_merge"


def layer_norm(x, g, b):
    xf = x.astype(jnp.float32)
    mu = jnp.mean(xf, axis=-1, keepdims=True)
    var = jnp.mean(jnp.square(xf - mu), axis=-1, keepdims=True)
    y = (xf - mu) * lax.rsqrt(var + LN_EPS)
    return (y * g.astype(jnp.float32) + b.astype(jnp.float32)).astype(x.dtype)


def rotary(x):
    s, d = x.shape[1], x.shape[-1]
    half = d // 2
    inv = ROPE_BASE ** (-2.0 * jnp.arange(half, dtype=jnp.float32) / d)
    ang = jnp.arange(s, dtype=jnp.float32)[:, None] * inv[None, :]
    cos = jnp.cos(ang)[None, :, None, :]
    sin = jnp.sin(ang)[None, :, None, :]
    xf = x.astype(jnp.float32)
    x1, x2 = xf[..., :half], xf[..., half:]
    return jnp.concatenate([x1 * cos - x2 * sin, x2 * cos + x1 * sin], axis=-1)


def chunk_retention(q, k, v):
    b, s, h, dk = q.shape
    dv = v.shape[-1]
    nc = s // CHUNK
    log_g = jnp.log1p(-jnp.exp2(-5.0 - jnp.arange(h, dtype=jnp.float32)))
    qc = q.reshape(b, nc, CHUNK, h, dk)
    kc = (k * (dk ** -0.5)).reshape(b, nc, CHUNK, h, dk)
    vc = v.astype(jnp.float32).reshape(b, nc, CHUNK, h, dv)
    pos = jnp.arange(CHUNK, dtype=jnp.float32)
    intra_decay = jnp.exp(log_g[:, None, None] * jnp.abs(pos[:, None] - pos[None, :]))
    scores = jnp.einsum('bnihd,bnjhd->bnhij', qc, kc) * intra_decay[None, None]
    o_intra = jnp.einsum('bnhij,bnjhe->bnihe', scores, vc)
    k_decay = jnp.exp((CHUNK - 1 - pos)[:, None] * log_g[None, :])
    kv = jnp.einsum('bnjhd,bnjhe,jh->nbhde', kc, vc, k_decay)
    chunk_decay = jnp.exp(log_g * CHUNK)[None, :, None, None]

    def step(state, kv_c):
        return state * chunk_decay + kv_c, state

    _, s_prev = lax.scan(step, jnp.zeros((b, h, dk, dv), jnp.float32), kv)
    q_decay = jnp.exp((pos + 1.0)[:, None] * log_g[None, :])
    o_inter = jnp.einsum('bnihd,nbhde,ih->bnihe', qc, s_prev, q_decay)
    return (o_intra + o_inter).reshape(b, s, h, dv)


def head_group_norm(o, g, bias):
    b, s, h, dv = o.shape
    mu = jnp.mean(o, axis=-1, keepdims=True)
    var = jnp.mean(jnp.square(o - mu), axis=-1, keepdims=True)
    y = ((o - mu) * lax.rsqrt(var + LN_EPS)).reshape(b, s, h * dv)
    return y * g.astype(jnp.float32) + bias.astype(jnp.float32)


def causal_depthwise_conv(x, w, bias):
    s = x.shape[1]
    xp = jnp.pad(x, ((0, 0), (CONV_WIDTH - 1, 0), (0, 0)))
    return bias + sum(xp[:, t:t + s] * w[t] for t in range(CONV_WIDTH))


def rg_lru(x, w_a, b_a, w_x, b_x, lam):
    b, s, w = x.shape
    xb = x.reshape(b, s, LRU_BLOCKS, LRU_BLOCK_DIM)
    r = jax.nn.sigmoid(jnp.einsum('bshi,hij->bshj', xb, w_a).reshape(b, s, w) + b_a)
    i = jax.nn.sigmoid(jnp.einsum('bshi,hij->bshj', xb, w_x).reshape(b, s, w) + b_x)
    log_a = -RG_LRU_C * r.astype(jnp.float32) * jax.nn.softplus(-lam.astype(jnp.float32))
    a = jnp.exp(log_a)
    u = jnp.sqrt(-jnp.expm1(2.0 * log_a)) * (i * x).astype(jnp.float32)

    def combine(left, right):
        a1, b1 = left
        a2, b2 = right
        return a1 * a2, a2 * b1 + b2

    _, hs = lax.associative_scan(combine, (a, u), axis=1)
    return hs.astype(x.dtype)


def setup_inputs(seed: int = 0) -> dict:
    key = jax.random.key(seed)
    ks = jax.random.split(key, 20)
    f32 = jnp.float32
    d = D_MODEL
    x = jax.random.normal(ks[0], (BATCH, SEQ, d), f32)
    ln_in_g = 1.0 + 0.02 * jax.random.normal(ks[1], (d,), f32)
    ln_in_b = 0.02 * jax.random.normal(ks[2], (d,), f32)
    col_scale = np.concatenate([np.full(n, DEEPNORM_BETA if idx == 2 else 1.0, np.float32)
                                for idx, n in enumerate(IN_SPLITS)])
    w_in = jax.random.normal(ks[3], (DEPTH, d, IN_WIDTH), f32) * (d ** -0.5) * jnp.asarray(col_scale)
    b_merge = 0.01 * jax.random.normal(ks[4], (DEPTH, 2, d), f32)
    ret_gn_g = 1.0 + 0.02 * jax.random.normal(ks[5], (DEPTH, RET_V_WIDTH), f32)
    ret_gn_b = 0.02 * jax.random.normal(ks[6], (DEPTH, RET_V_WIDTH), f32)
    w_ret_proj = jax.random.normal(ks[7], (DEPTH, RET_V_WIDTH, d), f32) * (RET_V_WIDTH ** -0.5) * DEEPNORM_BETA
    conv_w = jax.random.normal(ks[8], (DEPTH, CONV_WIDTH, LRU_WIDTH), f32) * (CONV_WIDTH ** -0.5)
    conv_b = 0.01 * jax.random.normal(ks[9], (DEPTH, LRU_WIDTH), f32)
    w_rg_a = jax.random.normal(ks[10], (DEPTH, LRU_BLOCKS, LRU_BLOCK_DIM, LRU_BLOCK_DIM), f32) * (LRU_BLOCK_DIM ** -0.5)
    b_rg_a = 0.01 * jax.random.normal(ks[11], (DEPTH, LRU_WIDTH), f32)
    w_rg_x = jax.random.normal(ks[12], (DEPTH, LRU_BLOCKS, LRU_BLOCK_DIM, LRU_BLOCK_DIM), f32) * (LRU_BLOCK_DIM ** -0.5)
    b_rg_x = 0.01 * jax.random.normal(ks[13], (DEPTH, LRU_WIDTH), f32)
    a0 = jax.random.uniform(ks[14], (DEPTH, LRU_WIDTH), f32, minval=0.9, maxval=0.999)
    a_base = a0 ** (1.0 / RG_LRU_C)
    lru_lambda = jnp.log(a_base) - jnp.log1p(-a_base)
    w_lru_proj = jax.random.normal(ks[15], (DEPTH, LRU_WIDTH, d), f32) * (LRU_WIDTH ** -0.5) * DEEPNORM_BETA
    w_out = jax.random.normal(ks[16], (DEPTH, d, d), f32) * (d ** -0.5) * DEEPNORM_BETA
    ln_out_g = 1.0 + 0.02 * jax.random.normal(ks[17], (DEPTH, d), f32)
    ln_out_b = 0.02 * jax.random.normal(ks[18], (DEPTH, d), f32)
    return {"x": x, "ln_in_g": ln_in_g, "ln_in_b": ln_in_b, "w_in": w_in, "b_merge": b_merge,
            "ret_gn_g": ret_gn_g, "ret_gn_b": ret_gn_b, "w_ret_proj": w_ret_proj,
            "conv_w": conv_w, "conv_b": conv_b, "w_rg_a": w_rg_a, "b_rg_a": b_rg_a,
            "w_rg_x": w_rg_x, "b_rg_x": b_rg_x, "lru_lambda": lru_lambda, "w_lru_proj": w_lru_proj,
            "w_out": w_out, "ln_out_g": ln_out_g, "ln_out_b": ln_out_b}


def reference(x, ln_in_g, ln_in_b, w_in, b_merge, ret_gn_g, ret_gn_b, w_ret_proj, conv_w, conv_b,
              w_rg_a, b_rg_a, w_rg_x, b_rg_x, lru_lambda, w_lru_proj, w_out, ln_out_g, ln_out_b):
    b, s, d = x.shape
    offsets = [int(c) for c in np.cumsum(IN_SPLITS)[:-1]]
    h = layer_norm(x, ln_in_g, ln_in_b)
    for layer in range(DEPTH):
        proj = h @ w_in[layer]
        q, k, v, g_ret, x_lru, g_lru, m_ret, m_lru = jnp.split(proj, offsets, axis=-1)
        qr = rotary(q.reshape(b, s, RET_HEADS, RET_QK_DIM))
        kr = rotary(k.reshape(b, s, RET_HEADS, RET_QK_DIM))
        ret = chunk_retention(qr, kr, v.reshape(b, s, RET_HEADS, RET_V_DIM))
        ret = head_group_norm(ret, ret_gn_g[layer], ret_gn_b[layer]).astype(h.dtype)
        ret_branch = (jax.nn.silu(g_ret) * ret) @ w_ret_proj[layer]
        xc = causal_depthwise_conv(x_lru, conv_w[layer], conv_b[layer])
        hl = rg_lru(xc, w_rg_a[layer], b_rg_a[layer], w_rg_x[layer], b_rg_x[layer], lru_lambda[layer])
        lru_branch = (jax.nn.silu(g_lru) * hl) @ w_lru_proj[layer]
        merged = (jax.nn.sigmoid(m_ret + b_merge[layer, 0]) * ret_branch
                  + jax.nn.sigmoid(m_lru + b_merge[layer, 1]) * lru_branch)
        y = merged @ w_out[layer]
        h = layer_norm(DEEPNORM_ALPHA * h + y, ln_out_g[layer], ln_out_b[layer])
    return h
```

```python
import functools

import jax
import jax.numpy as jnp
import numpy as np
from jax import lax
from jax.experimental import pallas as pl
from jax.experimental.pallas import tpu as pltpu

D_MODEL = 1024
CHUNK = 64
RET_HEADS = 4
RET_QK_DIM = 128
RET_V_DIM = D_MODEL // RET_HEADS
RET_QK_WIDTH = RET_HEADS * RET_QK_DIM
RET_V_WIDTH = RET_HEADS * RET_V_DIM
LRU_WIDTH = D_MODEL
LRU_BLOCKS = 8
LRU_BLOCK_DIM = LRU_WIDTH // LRU_BLOCKS
CONV_WIDTH = 4
RG_LRU_C = 8.0
ROPE_BASE = 10000.0
LN_EPS = 1e-5
DEPTH = 1
DEEPNORM_ALPHA = (2.0 * DEPTH) ** 0.25

OFF_Q = 0
OFF_K = OFF_Q + RET_QK_WIDTH
OFF_V = OFF_K + RET_QK_WIDTH
OFF_GRET = OFF_V + RET_V_WIDTH
OFF_XLRU = OFF_GRET + RET_V_WIDTH
OFF_GLRU = OFF_XLRU + LRU_WIDTH
OFF_MRET = OFF_GLRU + LRU_WIDTH
OFF_MLRU = OFF_MRET + D_MODEL
IN_WIDTH = OFF_MLRU + D_MODEL

SUBLANES = 8
LANES = 128
VMEM_LIMIT_BYTES = 58 * 1024 * 1024

TOKEN_BLOCK = 256

BF16 = jnp.bfloat16
F32 = jnp.float32


def _dot(a, b):
    return jnp.dot(a, b, preferred_element_type=F32)


def _layer_norm(x, g, b):
    mu = jnp.mean(x, axis=-1, keepdims=True)
    xc = x - mu
    var = jnp.mean(xc * xc, axis=-1, keepdims=True)
    return xc * lax.rsqrt(var + LN_EPS) * g + b


def _sigmoid(x):
    return 1.0 / (1.0 + jnp.exp(-x))


def _rotate_half(x, cos2, sin2):
    return x * cos2 + pltpu.roll(x, RET_QK_DIM // 2, axis=1) * sin2


def _fused_kernel(x_ref, cosq_ref, sinq_ref, cosk_ref, sink_ref, mask_ref, qdec_ref, kdec_ref,
                  ln_in_g_ref, ln_in_b_ref, w_in_ref, b_merge_ref, gn_g_ref, gn_b_ref, w_ret_ref,
                  conv_w_ref, conv_b_ref, w_a_ref, b_a_ref, w_x_ref, b_x_ref, lam_ref,
                  w_lru_ref, w_out_ref, ln_out_g_ref, ln_out_b_ref,
                  o_ref,
                  state_ref, hcarry_ref, xbuf_ref, hb_ref, z_ref, au_ref, *, block_decay):
    t = pl.program_id(1)
    tb = x_ref.shape[1]

    @pl.when(t == 0)
    def _():
        state_ref[...] = jnp.zeros_like(state_ref)
        hcarry_ref[...] = jnp.zeros_like(hcarry_ref)
        xbuf_ref[0:SUBLANES, :] = jnp.zeros((SUBLANES, LRU_WIDTH), F32)

    h = _layer_norm(x_ref[0], ln_in_g_ref[...], ln_in_b_ref[...])
    hb_ref[...] = h.astype(BF16)

    def proj(off, width):
        return _dot(hb_ref[...], w_in_ref[:, off:off + width])

    for hd in range(RET_HEADS):
        q = proj(OFF_Q + hd * RET_QK_DIM, RET_QK_DIM)
        k = proj(OFF_K + hd * RET_QK_DIM, RET_QK_DIM)
        v = proj(OFF_V + hd * RET_V_DIM, RET_V_DIM).astype(BF16)
        qr = _rotate_half(q, cosq_ref[...], sinq_ref[...])
        kr = _rotate_half(k, cosk_ref[...], sink_ref[...])
        scores = lax.dot_general(qr.astype(BF16), kr.astype(BF16), (((1,), (1,)), ((), ())),
                                 preferred_element_type=F32)
        p = (scores * mask_ref[hd]).astype(BF16)
        s_old = state_ref[hd]
        o = _dot(p, v) + _dot((qr * qdec_ref[hd]).astype(BF16), s_old.astype(BF16))
        kv = lax.dot_general((kr * kdec_ref[hd]).astype(BF16), v, (((0,), (0,)), ((), ())),
                             preferred_element_type=F32)
        state_ref[hd] = s_old * block_decay[hd] + kv
        sl = slice(hd * RET_V_DIM, (hd + 1) * RET_V_DIM)
        y = _layer_norm(o, gn_g_ref[:, sl], gn_b_ref[:, sl])
        g = proj(OFF_GRET + hd * RET_V_DIM, RET_V_DIM)
        z_ref[:, sl] = (g * _sigmoid(g) * y).astype(BF16)
    ret_branch = _dot(z_ref[...], w_ret_ref[...])

    xl = proj(OFF_XLRU, LRU_WIDTH)
    xbuf_ref[SUBLANES:SUBLANES + tb, :] = xl
    xc = conv_b_ref[...] + xl * conv_w_ref[CONV_WIDTH - 1:CONV_WIDTH, :]
    for back in range(1, CONV_WIDTH):
        tap = CONV_WIDTH - 1 - back
        xc = xc + xbuf_ref[SUBLANES - back:SUBLANES - back + tb, :] * conv_w_ref[tap:tap + 1, :]
    xbuf_ref[0:SUBLANES, :] = xl[tb - SUBLANES:, :]
    xcb = xc.astype(BF16)
    ra = []
    ri = []
    for blk in range(LRU_BLOCKS):
        sl = slice(blk * LRU_BLOCK_DIM, (blk + 1) * LRU_BLOCK_DIM)
        ra.append(_dot(xcb[:, sl], w_a_ref[blk]))
        ri.append(_dot(xcb[:, sl], w_x_ref[blk]))
    r = _sigmoid(jnp.concatenate(ra, axis=1) + b_a_ref[...])
    ig = _sigmoid(jnp.concatenate(ri, axis=1) + b_x_ref[...])
    neg_lam = -lam_ref[...]
    softplus = jnp.maximum(neg_lam, 0.0) + jnp.log1p(jnp.exp(-jnp.abs(neg_lam)))
    log_a = (-RG_LRU_C) * r * softplus
    a = jnp.exp(log_a)
    u = jnp.sqrt(-jnp.tanh(log_a) * (a * a + 1.0)) * (ig * xc)

    groups = tb // SUBLANES
    a3 = a.reshape(groups, SUBLANES, LRU_WIDTH)
    u3 = u.reshape(groups, SUBLANES, LRU_WIDTH)
    row = lax.broadcasted_iota(jnp.int32, (groups, SUBLANES, LRU_WIDTH), 1)
    step = 1
    while step < SUBLANES:
        valid = row >= step
        a_prev = jnp.where(valid, pltpu.roll(a3, step, axis=1), 1.0)
        u_prev = jnp.where(valid, pltpu.roll(u3, step, axis=1), 0.0)
        u3 = u3 + a3 * u_prev
        a3 = a3 * a_prev
        step *= 2
    au_ref[0] = a3.reshape(tb, LRU_WIDTH)
    au_ref[1] = u3.reshape(tb, LRU_WIDTH)
    carry = hcarry_ref[...]
    for g in range(groups):
        rows = slice(g * SUBLANES, (g + 1) * SUBLANES)
        hg = au_ref[1, rows, :] + au_ref[0, rows, :] * carry
        au_ref[1, rows, :] = hg
        carry = hg[SUBLANES - 1:SUBLANES, :]
    hcarry_ref[...] = carry
    hl = au_ref[1]
    gl = proj(OFF_GLRU, LRU_WIDTH)
    lru_branch = _dot((gl * _sigmoid(gl) * hl).astype(BF16), w_lru_ref[...])

    m_ret = proj(OFF_MRET, D_MODEL)
    m_lru = proj(OFF_MLRU, D_MODEL)
    merged = (_sigmoid(m_ret + b_merge_ref[0:1, :]) * ret_branch
              + _sigmoid(m_lru + b_merge_ref[1:2, :]) * lru_branch)
    y = _dot(merged.astype(BF16), w_out_ref[...])
    h = _layer_norm(x_ref[0], ln_in_g_ref[...], ln_in_b_ref[...])
    o_ref[0] = _layer_norm(DEEPNORM_ALPHA * h + y, ln_out_g_ref[...], ln_out_b_ref[...])


def _retention_tables(tb):
    log_g = np.log1p(-np.exp2(-5.0 - np.arange(RET_HEADS, dtype=np.float64)))
    pos = np.arange(tb, dtype=np.float64)
    diff = pos[:, None] - pos[None, :]
    ci = (np.arange(tb) // CHUNK)[:, None]
    cj = (np.arange(tb) // CHUNK)[None, :]
    expo = np.where(ci == cj, np.abs(diff), diff)
    mask = np.where((cj <= ci)[None], np.exp(log_g[:, None, None] * expo[None]), 0.0)
    qdec = np.exp(log_g[:, None] * (pos + 1.0)[None, :])
    kdec = np.exp(log_g[:, None] * (tb - 1.0 - pos)[None, :])
    qdec = np.broadcast_to(qdec[:, :, None], (RET_HEADS, tb, RET_QK_DIM))
    kdec = np.broadcast_to(kdec[:, :, None], (RET_HEADS, tb, RET_QK_DIM))
    block_decay = tuple(float(d) for d in np.exp(log_g * tb))
    return (mask.astype(np.float32), np.ascontiguousarray(qdec, dtype=np.float32),
            np.ascontiguousarray(kdec, dtype=np.float32), block_decay)


def _rotary_tables(seq):
    half = RET_QK_DIM // 2
    inv = ROPE_BASE ** (-2.0 * jnp.arange(half, dtype=F32) / RET_QK_DIM)
    ang = jnp.arange(seq, dtype=F32)[:, None] * inv[None, :]
    cos = jnp.cos(ang)
    sin = jnp.sin(ang)
    cos2 = jnp.concatenate([cos, cos], axis=-1)
    sin2 = jnp.concatenate([-sin, sin], axis=-1)
    k_scale = RET_QK_DIM ** -0.5
    return cos2, sin2, cos2 * k_scale, sin2 * k_scale


def _resident(shape):
    zeros = (0,) * len(shape)
    return pl.BlockSpec(shape, lambda b, t: zeros, pipeline_mode=pl.Buffered(1))


def kernel(x, ln_in_g, ln_in_b, w_in, b_merge, ret_gn_g, ret_gn_b, w_ret_proj, conv_w, conv_b,
           w_rg_a, b_rg_a, w_rg_x, b_rg_x, lru_lambda, w_lru_proj, w_out, ln_out_g, ln_out_b):
    batch, seq, d = x.shape
    assert d == D_MODEL and w_in.shape == (DEPTH, D_MODEL, IN_WIDTH)
    tb = TOKEN_BLOCK
    assert seq % tb == 0 and tb % CHUNK == 0

    mask, qdec, kdec, block_decay = _retention_tables(tb)
    cosq, sinq, cosk, sink = _rotary_tables(seq)
    row = lambda v: v.reshape(1, -1).astype(F32)

    operands = [
        x, cosq, sinq, cosk, sink, jnp.asarray(mask), jnp.asarray(qdec), jnp.asarray(kdec),
        row(ln_in_g), row(ln_in_b), w_in[0].astype(BF16), b_merge[0], row(ret_gn_g[0]),
        row(ret_gn_b[0]), w_ret_proj[0].astype(BF16), conv_w[0], row(conv_b[0]),
        w_rg_a[0].astype(BF16), row(b_rg_a[0]), w_rg_x[0].astype(BF16), row(b_rg_x[0]),
        row(lru_lambda[0]), w_lru_proj[0].astype(BF16), w_out[0].astype(BF16),
        row(ln_out_g[0]), row(ln_out_b[0]),
    ]
    rot_spec = pl.BlockSpec((tb, RET_QK_DIM), lambda b, t: (t, 0))
    in_specs = [pl.BlockSpec((1, tb, d), lambda b, t: (b, t, 0)), rot_spec, rot_spec, rot_spec, rot_spec]
    in_specs += [_resident(op.shape) for op in operands[5:]]

    return pl.pallas_call(
        functools.partial(_fused_kernel, block_decay=block_decay),
        out_shape=jax.ShapeDtypeStruct(x.shape, x.dtype),
        grid=(batch, seq // tb),
        in_specs=in_specs,
        out_specs=pl.BlockSpec((1, tb, d), lambda b, t: (b, t, 0)),
        scratch_shapes=[
            pltpu.VMEM((RET_HEADS, RET_QK_DIM, RET_V_DIM), F32),
            pltpu.VMEM((1, LRU_WIDTH), F32),
            pltpu.VMEM((SUBLANES + tb, LRU_WIDTH), F32),
            pltpu.VMEM((tb, D_MODEL), BF16),
            pltpu.VMEM((tb, RET_V_WIDTH), BF16),
            pltpu.VMEM((2, tb, LRU_WIDTH), F32),
        ],
        compiler_params=pltpu.CompilerParams(
            dimension_semantics=("arbitrary", "arbitrary"),
            vmem_limit_bytes=VMEM_LIMIT_BYTES,
        ),
        name="retention_rglru_fused",
    )(*operands)
```

```python
import functools
import math

import jax
import jax.numpy as jnp
import numpy as np
from jax import lax
from jax.experimental import pallas as pl
from jax.experimental.pallas import tpu as pltpu

D_MODEL = 1024
CHUNK = 64
RET_HEADS = 4
RET_QK_DIM = 128
RET_V_DIM = D_MODEL // RET_HEADS
RET_QK_WIDTH = RET_HEADS * RET_QK_DIM
RET_V_WIDTH = RET_HEADS * RET_V_DIM
LRU_WIDTH = D_MODEL
LRU_BLOCKS = 8
LRU_BLOCK_DIM = LRU_WIDTH // LRU_BLOCKS
CONV_WIDTH = 4
RG_LRU_C = 8.0
ROPE_BASE = 10000.0
LN_EPS = 1e-5
DEPTH = 1
DEEPNORM_ALPHA = (2.0 * DEPTH) ** 0.25

OFF_Q = 0
OFF_K = OFF_Q + RET_QK_WIDTH
OFF_V = OFF_K + RET_QK_WIDTH
OFF_GRET = OFF_V + RET_V_WIDTH
OFF_XLRU = OFF_GRET + RET_V_WIDTH
OFF_GLRU = OFF_XLRU + LRU_WIDTH
OFF_MRET = OFF_GLRU + LRU_WIDTH
OFF_MLRU = OFF_MRET + D_MODEL
IN_WIDTH = OFF_MLRU + D_MODEL

SUBLANES = 8
LANES = 128
VMEM_LIMIT_BYTES = 58 * 1024 * 1024

TOKEN_BLOCK = 256
LRU_SLICE = 2 * LRU_BLOCK_DIM
STAGE_ROWS_IN = 64
STAGE_ROWS_SQ = 256

BF16 = jnp.bfloat16
F32 = jnp.float32
NEG_LOG2E = -math.log2(math.e)


def _dot(a, b):
    return jnp.dot(a, b, preferred_element_type=F32)


def _layer_norm(x, g, b):
    mu = jnp.mean(x, axis=-1, keepdims=True)
    xc = x - mu
    var = jnp.mean(xc * xc, axis=-1, keepdims=True)
    return xc * lax.rsqrt(var + LN_EPS) * g + b


def _sigmoid(x):
    return 1.0 / (1.0 + jnp.exp2(x * NEG_LOG2E))


def _rotate_half(x, cos2, sin2):
    return x * cos2 + pltpu.roll(x, RET_QK_DIM // 2, axis=1) * sin2


def _stage_weight(src_hbm, dst_ref, stage_ref, sem_ref, rows):
    n_rows, width = src_hbm.shape
    n_chunks = n_rows // rows

    def copy(chunk, slot):
        return pltpu.make_async_copy(src_hbm.at[pl.ds(chunk * rows, rows), :],
                                     stage_ref.at[slot, :, pl.ds(0, width)], sem_ref.at[slot])

    copy(0, 0).start()

    def body(chunk, carry):
        slot = lax.rem(chunk, 2)

        @pl.when(chunk + 1 < n_chunks)
        def _():
            copy(chunk + 1, 1 - slot).start()

        copy(chunk, slot).wait()
        row0 = pl.multiple_of(chunk * rows, rows)
        dst_ref[pl.ds(row0, rows), :] = stage_ref[slot, :, 0:width].astype(BF16)
        return carry

    lax.fori_loop(0, n_chunks, body, 0)


def _fused_kernel(x_ref, cosq_ref, sinq_ref, cosk_ref, sink_ref, mask_ref, qdec_ref, kdec_ref,
                  ln_in_g_ref, ln_in_b_ref, b_merge_ref, gn_g_ref, gn_b_ref,
                  conv_w_ref, conv_b_ref, w_gate_ref, b_a_ref, b_x_ref, lam_ref,
                  ln_out_g_ref, ln_out_b_ref,
                  w_in_hbm, w_ret_hbm, w_lru_hbm, w_out_hbm,
                  o_ref,
                  w_in_ref, w_ret_ref, w_lru_ref, w_out_ref, stage_in_ref, stage_sq_ref, sem_ref,
                  state_ref, hcarry_ref, xbuf_ref, h_ref, z_ref, zl_ref, *, block_decay):
    b = pl.program_id(0)
    t = pl.program_id(1)
    tb = x_ref.shape[1]

    @pl.when((b == 0) & (t == 0))
    def _():
        _stage_weight(w_in_hbm, w_in_ref, stage_in_ref, sem_ref, STAGE_ROWS_IN)
        _stage_weight(w_ret_hbm, w_ret_ref, stage_sq_ref, sem_ref, STAGE_ROWS_SQ)
        _stage_weight(w_lru_hbm, w_lru_ref, stage_sq_ref, sem_ref, STAGE_ROWS_SQ)
        _stage_weight(w_out_hbm, w_out_ref, stage_sq_ref, sem_ref, STAGE_ROWS_SQ)

    @pl.when(t == 0)
    def _():
        state_ref[...] = jnp.zeros_like(state_ref)
        hcarry_ref[...] = jnp.zeros_like(hcarry_ref)
        xbuf_ref[0:SUBLANES, :] = jnp.zeros((SUBLANES, LRU_WIDTH), F32)

    h = _layer_norm(x_ref[0], ln_in_g_ref[...], ln_in_b_ref[...])
    h_ref[...] = h
    hb = h.astype(BF16)

    def proj(off, width):
        return _dot(hb, w_in_ref[:, off:off + width])

    q_all = proj(OFF_Q, RET_QK_WIDTH)
    k_all = proj(OFF_K, RET_QK_WIDTH)

    xl = proj(OFF_XLRU, LRU_WIDTH)
    xbuf_ref[SUBLANES:SUBLANES + tb, :] = xl
    xc = conv_b_ref[...] + xl * conv_w_ref[CONV_WIDTH - 1:CONV_WIDTH, :]
    for back in range(1, CONV_WIDTH):
        tap = CONV_WIDTH - 1 - back
        xc = xc + xbuf_ref[SUBLANES - back:SUBLANES - back + tb, :] * conv_w_ref[tap:tap + 1, :]
    xbuf_ref[0:SUBLANES, :] = xl[tb - SUBLANES:, :]
    xcb = xc.astype(BF16)

    neg_lam = -lam_ref[...]
    softplus = jnp.maximum(neg_lam, 0.0) + jnp.log1p(jnp.exp(-jnp.abs(neg_lam)))
    groups = tb // SUBLANES
    row = lax.broadcasted_iota(jnp.int32, (groups, SUBLANES, LRU_SLICE), 1)

    for hd in range(RET_HEADS):
        qk = slice(hd * RET_QK_DIM, (hd + 1) * RET_QK_DIM)
        v = proj(OFF_V + hd * RET_V_DIM, RET_V_DIM).astype(BF16)
        qr = _rotate_half(q_all[:, qk], cosq_ref[...], sinq_ref[...])
        kr = _rotate_half(k_all[:, qk], cosk_ref[...], sink_ref[...])
        scores = lax.dot_general(qr.astype(BF16), kr.astype(BF16), (((1,), (1,)), ((), ())),
                                 preferred_element_type=F32)
        p = (scores * mask_ref[hd]).astype(BF16)
        s_old = state_ref[hd]
        o = _dot(p, v) + _dot((qr * qdec_ref[hd]).astype(BF16), s_old.astype(BF16))
        kv = lax.dot_general((kr * kdec_ref[hd]).astype(BF16), v, (((0,), (0,)), ((), ())),
                             preferred_element_type=F32)
        state_ref[hd] = s_old * block_decay[hd] + kv
        sl = slice(hd * RET_V_DIM, (hd + 1) * RET_V_DIM)
        y = _layer_norm(o, gn_g_ref[:, sl], gn_b_ref[:, sl])
        g = proj(OFF_GRET + hd * RET_V_DIM, RET_V_DIM)
        z_ref[:, sl] = (g * _sigmoid(g) * y).astype(BF16)

        cs = slice(hd * LRU_SLICE, (hd + 1) * LRU_SLICE)
        blk0 = hd * (LRU_SLICE // LRU_BLOCK_DIM)
        g0 = _dot(xcb[:, blk0 * LRU_BLOCK_DIM:(blk0 + 1) * LRU_BLOCK_DIM], w_gate_ref[blk0])
        g1 = _dot(xcb[:, (blk0 + 1) * LRU_BLOCK_DIM:(blk0 + 2) * LRU_BLOCK_DIM], w_gate_ref[blk0 + 1])
        ra = jnp.concatenate([g0[:, :LRU_BLOCK_DIM], g1[:, :LRU_BLOCK_DIM]], axis=1)
        ri = jnp.concatenate([g0[:, LRU_BLOCK_DIM:], g1[:, LRU_BLOCK_DIM:]], axis=1)
        r = _sigmoid(ra + b_a_ref[:, cs])
        ig = _sigmoid(ri + b_x_ref[:, cs])
        log_a = (-RG_LRU_C) * r * softplus[:, cs]
        a = jnp.exp(log_a)
        one_m_a2 = -jnp.tanh(log_a) * (a * a + 1.0)
        root = jnp.where(one_m_a2 > 0.0, one_m_a2 * lax.rsqrt(one_m_a2), 0.0)
        u = root * (ig * xc[:, cs])
        a3 = a.reshape(groups, SUBLANES, LRU_SLICE)
        u3 = u.reshape(groups, SUBLANES, LRU_SLICE)
        step = 1
        while step < SUBLANES:
            valid = row >= step
            a_prev = jnp.where(valid, pltpu.roll(a3, step, axis=1), 1.0)
            u_prev = jnp.where(valid, pltpu.roll(u3, step, axis=1), 0.0)
            u3 = u3 + a3 * u_prev
            a3 = a3 * a_prev
            step *= 2
        carry = hcarry_ref[:, cs]
        hs = []
        for grp in range(groups):
            hg = u3[grp] + a3[grp] * carry
            hs.append(hg)
            carry = hg[SUBLANES - 1:SUBLANES, :]
        hcarry_ref[:, cs] = carry
        hl = jnp.concatenate(hs, axis=0)
        gl = proj(OFF_GLRU + hd * LRU_SLICE, LRU_SLICE)
        zl_ref[:, cs] = (gl * _sigmoid(gl) * hl).astype(BF16)

    ret_branch = _dot(z_ref[...], w_ret_ref[...])
    lru_branch = _dot(zl_ref[...], w_lru_ref[...])
    m_ret = proj(OFF_MRET, D_MODEL)
    m_lru = proj(OFF_MLRU, D_MODEL)
    merged = (_sigmoid(m_ret + b_merge_ref[0:1, :]) * ret_branch
              + _sigmoid(m_lru + b_merge_ref[1:2, :]) * lru_branch)
    y = _dot(merged.astype(BF16), w_out_ref[...])
    o_ref[0] = _layer_norm(DEEPNORM_ALPHA * h_ref[...] + y, ln_out_g_ref[...], ln_out_b_ref[...])


def _retention_tables(tb):
    log_g = np.log1p(-np.exp2(-5.0 - np.arange(RET_HEADS, dtype=np.float64)))
    pos = np.arange(tb, dtype=np.float64)
    diff = pos[:, None] - pos[None, :]
    ci = (np.arange(tb) // CHUNK)[:, None]
    cj = (np.arange(tb) // CHUNK)[None, :]
    expo = np.where(ci == cj, np.abs(diff), diff)
    mask = np.where((cj <= ci)[None], np.exp(log_g[:, None, None] * expo[None]), 0.0)
    qdec = np.exp(log_g[:, None] * (pos + 1.0)[None, :])
    kdec = np.exp(log_g[:, None] * (tb - 1.0 - pos)[None, :])
    qdec = np.broadcast_to(qdec[:, :, None], (RET_HEADS, tb, RET_QK_DIM))
    kdec = np.broadcast_to(kdec[:, :, None], (RET_HEADS, tb, RET_QK_DIM))
    block_decay = tuple(float(d) for d in np.exp(log_g * tb))
    return (mask.astype(np.float32), np.ascontiguousarray(qdec, dtype=np.float32),
            np.ascontiguousarray(kdec, dtype=np.float32), block_decay)


def _rotary_tables(seq):
    half = RET_QK_DIM // 2
    inv = ROPE_BASE ** (-2.0 * jnp.arange(half, dtype=F32) / RET_QK_DIM)
    ang = jnp.arange(seq, dtype=F32)[:, None] * inv[None, :]
    cos = jnp.cos(ang)
    sin = jnp.sin(ang)
    cos2 = jnp.concatenate([cos, cos], axis=-1)
    sin2 = jnp.concatenate([-sin, sin], axis=-1)
    k_scale = RET_QK_DIM ** -0.5
    return cos2, sin2, cos2 * k_scale, sin2 * k_scale


def _resident(shape):
    zeros = (0,) * len(shape)
    return pl.BlockSpec(shape, lambda b, t: zeros, pipeline_mode=pl.Buffered(1))


def kernel(x, ln_in_g, ln_in_b, w_in, b_merge, ret_gn_g, ret_gn_b, w_ret_proj, conv_w, conv_b,
           w_rg_a, b_rg_a, w_rg_x, b_rg_x, lru_lambda, w_lru_proj, w_out, ln_out_g, ln_out_b):
    batch, seq, d = x.shape
    assert d == D_MODEL and w_in.shape == (DEPTH, D_MODEL, IN_WIDTH)
    tb = TOKEN_BLOCK
    assert seq % tb == 0 and tb % CHUNK == 0

    mask, qdec, kdec, block_decay = _retention_tables(tb)
    cosq, sinq, cosk, sink = _rotary_tables(seq)
    row = lambda v: v.reshape(1, -1).astype(F32)
    w_gate = jnp.concatenate([w_rg_a[0], w_rg_x[0]], axis=-1).astype(BF16)

    blocked = [x, cosq, sinq, cosk, sink]
    resident = [
        jnp.asarray(mask), jnp.asarray(qdec), jnp.asarray(kdec),
        row(ln_in_g), row(ln_in_b), b_merge[0], row(ret_gn_g[0]), row(ret_gn_b[0]),
        conv_w[0], row(conv_b[0]), w_gate, row(b_rg_a[0]), row(b_rg_x[0]), row(lru_lambda[0]),
        row(ln_out_g[0]), row(ln_out_b[0]),
    ]
    in_hbm = [w_in[0], w_ret_proj[0], w_lru_proj[0], w_out[0]]
    rot_spec = pl.BlockSpec((tb, RET_QK_DIM), lambda b, t: (t, 0))
    in_specs = [pl.BlockSpec((1, tb, d), lambda b, t: (b, t, 0)), rot_spec, rot_spec, rot_spec, rot_spec]
    in_specs += [_resident(op.shape) for op in resident]
    in_specs += [pl.BlockSpec(memory_space=pl.ANY)] * len(in_hbm)

    return pl.pallas_call(
        functools.partial(_fused_kernel, block_decay=block_decay),
        out_shape=jax.ShapeDtypeStruct(x.shape, x.dtype),
        grid=(batch, seq // tb),
        in_specs=in_specs,
        out_specs=pl.BlockSpec((1, tb, d), lambda b, t: (b, t, 0)),
        scratch_shapes=[
            pltpu.VMEM((D_MODEL, IN_WIDTH), BF16),
            pltpu.VMEM((RET_V_WIDTH, D_MODEL), BF16),
            pltpu.VMEM((LRU_WIDTH, D_MODEL), BF16),
            pltpu.VMEM((D_MODEL, D_MODEL), BF16),
            pltpu.VMEM((2, STAGE_ROWS_IN, IN_WIDTH), F32),
            pltpu.VMEM((2, STAGE_ROWS_SQ, D_MODEL), F32),
            pltpu.SemaphoreType.DMA((2,)),
            pltpu.VMEM((RET_HEADS, RET_QK_DIM, RET_V_DIM), F32),
            pltpu.VMEM((1, LRU_WIDTH), F32),
            pltpu.VMEM((SUBLANES + tb, LRU_WIDTH), F32),
            pltpu.VMEM((tb, D_MODEL), F32),
            pltpu.VMEM((tb, RET_V_WIDTH), BF16),
            pltpu.VMEM((tb, LRU_WIDTH), BF16),
        ],
        compiler_params=pltpu.CompilerParams(
            dimension_semantics=("arbitrary", "arbitrary"),
            vmem_limit_bytes=VMEM_LIMIT_BYTES,
        ),
        name="retention_rglru_fused",
    )(*blocked, *resident, *in_hbm)
```

```python
import functools
import math

import jax
import jax.numpy as jnp
import numpy as np
from jax import lax
from jax.experimental import pallas as pl
from jax.experimental.pallas import tpu as pltpu

D_MODEL = 1024
CHUNK = 64
RET_HEADS = 4
RET_QK_DIM = 128
RET_V_DIM = D_MODEL // RET_HEADS
RET_QK_WIDTH = RET_HEADS * RET_QK_DIM
RET_V_WIDTH = RET_HEADS * RET_V_DIM
LRU_WIDTH = D_MODEL
LRU_BLOCKS = 8
LRU_BLOCK_DIM = LRU_WIDTH // LRU_BLOCKS
CONV_WIDTH = 4
RG_LRU_C = 8.0
ROPE_BASE = 10000.0
LN_EPS = 1e-5
DEPTH = 1
DEEPNORM_ALPHA = (2.0 * DEPTH) ** 0.25

OFF_Q = 0
OFF_K = OFF_Q + RET_QK_WIDTH
OFF_V = OFF_K + RET_QK_WIDTH
OFF_GRET = OFF_V + RET_V_WIDTH
OFF_XLRU = OFF_GRET + RET_V_WIDTH
OFF_GLRU = OFF_XLRU + LRU_WIDTH
OFF_MRET = OFF_GLRU + LRU_WIDTH
OFF_MLRU = OFF_MRET + D_MODEL
IN_WIDTH = OFF_MLRU + D_MODEL

SUBLANES = 8
LANES = 128
VMEM_LIMIT_BYTES = 58 * 1024 * 1024

TOKEN_BLOCK = 512
RET_BLOCK = 256
LRU_SLICE = 2 * LRU_BLOCK_DIM
STAGE_ROWS_IN = 64
STAGE_ROWS_SQ = 256

BF16 = jnp.bfloat16
F32 = jnp.float32
NEG_LOG2E = -math.log2(math.e)


def _dot(a, b):
    return jnp.dot(a, b, preferred_element_type=F32)


def _layer_norm(x, g, b):
    mu = jnp.mean(x, axis=-1, keepdims=True)
    xc = x - mu
    var = jnp.mean(xc * xc, axis=-1, keepdims=True)
    return xc * lax.rsqrt(var + LN_EPS) * g + b


def _sigmoid(x):
    return 1.0 / (1.0 + jnp.exp2(x * NEG_LOG2E))


def _rotate_half(x, cos2, sin2):
    return x * cos2 + pltpu.roll(x, RET_QK_DIM // 2, axis=1) * sin2


def _stage_weight(src_hbm, dst_ref, stage_ref, sem_ref, rows):
    n_rows, width = src_hbm.shape
    n_chunks = n_rows // rows

    def copy(chunk, slot):
        return pltpu.make_async_copy(src_hbm.at[pl.ds(chunk * rows, rows), :],
                                     stage_ref.at[slot, :, pl.ds(0, width)], sem_ref.at[slot])

    copy(0, 0).start()

    def body(chunk, carry):
        slot = lax.rem(chunk, 2)

        @pl.when(chunk + 1 < n_chunks)
        def _():
            copy(chunk + 1, 1 - slot).start()

        copy(chunk, slot).wait()
        row0 = pl.multiple_of(chunk * rows, rows)
        dst_ref[pl.ds(row0, rows), :] = stage_ref[slot, :, 0:width].astype(BF16)
        return carry

    lax.fori_loop(0, n_chunks, body, 0)


def _fused_kernel(x_ref, cosq_ref, sinq_ref, cosk_ref, sink_ref, mask_ref, qdec_ref, kdec_ref,
                  ln_in_g_ref, ln_in_b_ref, b_merge_ref, gn_g_ref, gn_b_ref,
                  conv_w_ref, conv_b_ref, w_gate_ref, b_a_ref, b_x_ref, lam_ref,
                  ln_out_g_ref, ln_out_b_ref,
                  w_in_hbm, w_ret_hbm, w_lru_hbm, w_out_hbm,
                  o_ref,
                  w_in_ref, w_ret_ref, w_lru_ref, w_out_ref, stage_in_ref, stage_sq_ref, sem_ref,
                  state_ref, hcarry_ref, xbuf_ref, h_ref, z_ref, zl_ref, *, block_decay):
    b = pl.program_id(0)
    t = pl.program_id(1)
    tb = x_ref.shape[1]

    @pl.when((b == 0) & (t == 0))
    def _():
        _stage_weight(w_in_hbm, w_in_ref, stage_in_ref, sem_ref, STAGE_ROWS_IN)
        _stage_weight(w_ret_hbm, w_ret_ref, stage_sq_ref, sem_ref, STAGE_ROWS_SQ)
        _stage_weight(w_lru_hbm, w_lru_ref, stage_sq_ref, sem_ref, STAGE_ROWS_SQ)
        _stage_weight(w_out_hbm, w_out_ref, stage_sq_ref, sem_ref, STAGE_ROWS_SQ)

    @pl.when(t == 0)
    def _():
        state_ref[...] = jnp.zeros_like(state_ref)
        hcarry_ref[...] = jnp.zeros_like(hcarry_ref)
        xbuf_ref[0:SUBLANES, :] = jnp.zeros((SUBLANES, LRU_WIDTH), F32)

    h = _layer_norm(x_ref[0], ln_in_g_ref[...], ln_in_b_ref[...])
    h_ref[...] = h
    hb = h.astype(BF16)

    def proj(off, width):
        return _dot(hb, w_in_ref[:, off:off + width])

    q_all = proj(OFF_Q, RET_QK_WIDTH)
    k_all = proj(OFF_K, RET_QK_WIDTH)

    xl = proj(OFF_XLRU, LRU_WIDTH)
    xbuf_ref[SUBLANES:SUBLANES + tb, :] = xl
    xc = conv_b_ref[...] + xl * conv_w_ref[CONV_WIDTH - 1:CONV_WIDTH, :]
    for back in range(1, CONV_WIDTH):
        tap = CONV_WIDTH - 1 - back
        xc = xc + xbuf_ref[SUBLANES - back:SUBLANES - back + tb, :] * conv_w_ref[tap:tap + 1, :]
    xbuf_ref[0:SUBLANES, :] = xl[tb - SUBLANES:, :]
    xcb = xc.astype(BF16)

    neg_lam = -lam_ref[...]
    softplus = jnp.maximum(neg_lam, 0.0) + jnp.log1p(jnp.exp(-jnp.abs(neg_lam)))
    groups = tb // SUBLANES
    row = lax.broadcasted_iota(jnp.int32, (groups, SUBLANES, LRU_SLICE), 1)

    for hd in range(RET_HEADS):
        qk = slice(hd * RET_QK_DIM, (hd + 1) * RET_QK_DIM)
        sl = slice(hd * RET_V_DIM, (hd + 1) * RET_V_DIM)
        v_all = proj(OFF_V + hd * RET_V_DIM, RET_V_DIM).astype(BF16)
        g_all = proj(OFF_GRET + hd * RET_V_DIM, RET_V_DIM)
        for sub in range(tb // RET_BLOCK):
            rows = slice(sub * RET_BLOCK, (sub + 1) * RET_BLOCK)
            v = v_all[rows]
            qr = _rotate_half(q_all[rows, qk], cosq_ref[rows, :], sinq_ref[rows, :])
            kr = _rotate_half(k_all[rows, qk], cosk_ref[rows, :], sink_ref[rows, :])
            scores = lax.dot_general(qr.astype(BF16), kr.astype(BF16), (((1,), (1,)), ((), ())),
                                     preferred_element_type=F32)
            p = (scores * mask_ref[hd]).astype(BF16)
            s_old = state_ref[hd]
            o = _dot(p, v) + _dot((qr * qdec_ref[hd]).astype(BF16), s_old.astype(BF16))
            kv = lax.dot_general((kr * kdec_ref[hd]).astype(BF16), v, (((0,), (0,)), ((), ())),
                                 preferred_element_type=F32)
            state_ref[hd] = s_old * block_decay[hd] + kv
            y = _layer_norm(o, gn_g_ref[:, sl], gn_b_ref[:, sl])
            g = g_all[rows]
            z_ref[rows, sl] = (g * _sigmoid(g) * y).astype(BF16)

        cs = slice(hd * LRU_SLICE, (hd + 1) * LRU_SLICE)
        blk0 = hd * (LRU_SLICE // LRU_BLOCK_DIM)
        g0 = _dot(xcb[:, blk0 * LRU_BLOCK_DIM:(blk0 + 1) * LRU_BLOCK_DIM], w_gate_ref[blk0])
        g1 = _dot(xcb[:, (blk0 + 1) * LRU_BLOCK_DIM:(blk0 + 2) * LRU_BLOCK_DIM], w_gate_ref[blk0 + 1])
        ra = jnp.concatenate([g0[:, :LRU_BLOCK_DIM], g1[:, :LRU_BLOCK_DIM]], axis=1)
        ri = jnp.concatenate([g0[:, LRU_BLOCK_DIM:], g1[:, LRU_BLOCK_DIM:]], axis=1)
        r = _sigmoid(ra + b_a_ref[:, cs])
        ig = _sigmoid(ri + b_x_ref[:, cs])
        log_a = (-RG_LRU_C) * r * softplus[:, cs]
        a = jnp.exp(log_a)
        one_m_a2 = -jnp.tanh(log_a) * (a * a + 1.0)
        root = jnp.where(one_m_a2 > 0.0, one_m_a2 * lax.rsqrt(one_m_a2), 0.0)
        u = root * (ig * xc[:, cs])
        a3 = a.reshape(groups, SUBLANES, LRU_SLICE)
        u3 = u.reshape(groups, SUBLANES, LRU_SLICE)
        step = 1
        while step < SUBLANES:
            valid = row >= step
            a_prev = jnp.where(valid, pltpu.roll(a3, step, axis=1), 1.0)
            u_prev = jnp.where(valid, pltpu.roll(u3, step, axis=1), 0.0)
            u3 = u3 + a3 * u_prev
            a3 = a3 * a_prev
            step *= 2
        carry = hcarry_ref[:, cs]
        hs = []
        for grp in range(groups):
            hg = u3[grp] + a3[grp] * carry
            hs.append(hg)
            carry = hg[SUBLANES - 1:SUBLANES, :]
        hcarry_ref[:, cs] = carry
        hl = jnp.concatenate(hs, axis=0)
        gl = proj(OFF_GLRU + hd * LRU_SLICE, LRU_SLICE)
        zl_ref[:, cs] = (gl * _sigmoid(gl) * hl).astype(BF16)

    ret_branch = _dot(z_ref[...], w_ret_ref[...])
    lru_branch = _dot(zl_ref[...], w_lru_ref[...])
    m_ret = proj(OFF_MRET, D_MODEL)
    m_lru = proj(OFF_MLRU, D_MODEL)
    merged = (_sigmoid(m_ret + b_merge_ref[0:1, :]) * ret_branch
              + _sigmoid(m_lru + b_merge_ref[1:2, :]) * lru_branch)
    y = _dot(merged.astype(BF16), w_out_ref[...])
    o_ref[0] = _layer_norm(DEEPNORM_ALPHA * h_ref[...] + y, ln_out_g_ref[...], ln_out_b_ref[...])


def _retention_tables(tb):
    log_g = np.log1p(-np.exp2(-5.0 - np.arange(RET_HEADS, dtype=np.float64)))
    pos = np.arange(tb, dtype=np.float64)
    diff = pos[:, None] - pos[None, :]
    ci = (np.arange(tb) // CHUNK)[:, None]
    cj = (np.arange(tb) // CHUNK)[None, :]
    expo = np.where(ci == cj, np.abs(diff), diff)
    mask = np.where((cj <= ci)[None], np.exp(log_g[:, None, None] * expo[None]), 0.0)
    qdec = np.exp(log_g[:, None] * (pos + 1.0)[None, :])
    kdec = np.exp(log_g[:, None] * (tb - 1.0 - pos)[None, :])
    qdec = np.broadcast_to(qdec[:, :, None], (RET_HEADS, tb, RET_QK_DIM))
    kdec = np.broadcast_to(kdec[:, :, None], (RET_HEADS, tb, RET_QK_DIM))
    block_decay = tuple(float(d) for d in np.exp(log_g * tb))
    return (mask.astype(np.float32), np.ascontiguousarray(qdec, dtype=np.float32),
            np.ascontiguousarray(kdec, dtype=np.float32), block_decay)


def _rotary_tables(seq):
    half = RET_QK_DIM // 2
    inv = ROPE_BASE ** (-2.0 * jnp.arange(half, dtype=F32) / RET_QK_DIM)
    ang = jnp.arange(seq, dtype=F32)[:, None] * inv[None, :]
    cos = jnp.cos(ang)
    sin = jnp.sin(ang)
    cos2 = jnp.concatenate([cos, cos], axis=-1)
    sin2 = jnp.concatenate([-sin, sin], axis=-1)
    k_scale = RET_QK_DIM ** -0.5
    return cos2, sin2, cos2 * k_scale, sin2 * k_scale


def _resident(shape):
    zeros = (0,) * len(shape)
    return pl.BlockSpec(shape, lambda b, t: zeros, pipeline_mode=pl.Buffered(1))


def kernel(x, ln_in_g, ln_in_b, w_in, b_merge, ret_gn_g, ret_gn_b, w_ret_proj, conv_w, conv_b,
           w_rg_a, b_rg_a, w_rg_x, b_rg_x, lru_lambda, w_lru_proj, w_out, ln_out_g, ln_out_b):
    batch, seq, d = x.shape
    assert d == D_MODEL and w_in.shape == (DEPTH, D_MODEL, IN_WIDTH)
    tb = TOKEN_BLOCK
    assert seq % tb == 0 and tb % RET_BLOCK == 0 and RET_BLOCK % CHUNK == 0

    mask, qdec, kdec, block_decay = _retention_tables(RET_BLOCK)
    cosq, sinq, cosk, sink = _rotary_tables(seq)
    row = lambda v: v.reshape(1, -1).astype(F32)
    w_gate = jnp.concatenate([w_rg_a[0], w_rg_x[0]], axis=-1).astype(BF16)

    blocked = [x, cosq, sinq, cosk, sink]
    resident = [
        jnp.asarray(mask), jnp.asarray(qdec), jnp.asarray(kdec),
        row(ln_in_g), row(ln_in_b), b_merge[0], row(ret_gn_g[0]), row(ret_gn_b[0]),
        conv_w[0], row(conv_b[0]), w_gate, row(b_rg_a[0]), row(b_rg_x[0]), row(lru_lambda[0]),
        row(ln_out_g[0]), row(ln_out_b[0]),
    ]
    in_hbm = [w_in[0], w_ret_proj[0], w_lru_proj[0], w_out[0]]
    rot_spec = pl.BlockSpec((tb, RET_QK_DIM), lambda b, t: (t, 0))
    in_specs = [pl.BlockSpec((1, tb, d), lambda b, t: (b, t, 0)), rot_spec, rot_spec, rot_spec, rot_spec]
    in_specs += [_resident(op.shape) for op in resident]
    in_specs += [pl.BlockSpec(memory_space=pl.ANY)] * len(in_hbm)

    return pl.pallas_call(
        functools.partial(_fused_kernel, block_decay=block_decay),
        out_shape=jax.ShapeDtypeStruct(x.shape, x.dtype),
        grid=(batch, seq // tb),
        in_specs=in_specs,
        out_specs=pl.BlockSpec((1, tb, d), lambda b, t: (b, t, 0)),
        scratch_shapes=[
            pltpu.VMEM((D_MODEL, IN_WIDTH), BF16),
            pltpu.VMEM((RET_V_WIDTH, D_MODEL), BF16),
            pltpu.VMEM((LRU_WIDTH, D_MODEL), BF16),
            pltpu.VMEM((D_MODEL, D_MODEL), BF16),
            pltpu.VMEM((2, STAGE_ROWS_IN, IN_WIDTH), F32),
            pltpu.VMEM((2, STAGE_ROWS_SQ, D_MODEL), F32),
            pltpu.SemaphoreType.DMA((2,)),
            pltpu.VMEM((RET_HEADS, RET_QK_DIM, RET_V_DIM), F32),
            pltpu.VMEM((1, LRU_WIDTH), F32),
            pltpu.VMEM((SUBLANES + tb, LRU_WIDTH), F32),
            pltpu.VMEM((tb, D_MODEL), F32),
            pltpu.VMEM((tb, RET_V_WIDTH), BF16),
            pltpu.VMEM((tb, LRU_WIDTH), BF16),
        ],
        compiler_params=pltpu.CompilerParams(
            dimension_semantics=("arbitrary", "arbitrary"),
            vmem_limit_bytes=VMEM_LIMIT_BYTES,
        ),
        name="retention_rglru_fused",
    )(*blocked, *resident, *in_hbm)
```
